```python
import jax, jax.numpy as jnp
from jax import lax
import numpy as np

D_MODEL = 1024
BATCH = 16
SEQ = 2048
DEPTH = 1
DEC_BATCH = 16
DEC_SEQ = 64
PAST_LEN = 4096

CHUNK = 64
D_MIX = D_MODEL
N_HEADS = 8
HEAD_DIM = 64
ATTN_DIM = N_HEADS * HEAD_DIM
IDX_HEADS = 8
IDX_DIM = 64
TOPK_MAX = 256
CONV_DIM = D_MIX - ATTN_DIM
CONV_WIDTH = 31
D_FF = 4 * D_MODEL
Q_BLOCK = CHUNK
EPS = 1e-6
NEG_INF = -1e30

PROJ_SIZES = (ATTN_DIM, ATTN_DIM, ATTN_DIM, IDX_HEADS * IDX_DIM, IDX_DIM, IDX_HEADS, CONV_DIM, CONV_DIM)
PROJ_DIM = int(sum(PROJ_SIZES))
PROJ_SPLITS = tuple(int(v) for v in np.cumsum(PROJ_SIZES)[:-1])

kernel_name = "hybrid_dsa_conformer_stream_step"


def alibi_slopes():
    return jnp.asarray([2.0 ** (-8.0 * (h + 1) / N_HEADS) for h in range(N_HEADS)], jnp.float32)


def rms_norm(x, g):
    xf = x.astype(jnp.float32)
    y = xf * lax.rsqrt(jnp.mean(xf * xf, axis=-1, keepdims=True) + EPS)
    return (y * g.astype(jnp.float32)).astype(x.dtype)


def layer_norm(x, g, b):
    xf = x.astype(jnp.float32)
    mu = jnp.mean(xf, axis=-1, keepdims=True)
    xc = xf - mu
    var = jnp.mean(xc * xc, axis=-1, keepdims=True)
    return (xc * lax.rsqrt(var + EPS) * g.astype(jnp.float32) + b.astype(jnp.float32)).astype(x.dtype)


def ada_modulation(c, w_ada, b_ada):
    mod = jax.nn.silu(c) @ w_ada + b_ada
    return jnp.split(mod[:, None, :], 6, axis=-1)


def split_projection(h, w_in):
    b, t, _ = h.shape
    q, k, v, qi, ki, wi, ga, gb = jnp.split(h @ w_in, PROJ_SPLITS, axis=-1)
    q = q.reshape(b, t, N_HEADS, HEAD_DIM)
    k = k.reshape(b, t, N_HEADS, HEAD_DIM)
    v = v.reshape(b, t, N_HEADS, HEAD_DIM)
    qi = qi.reshape(b, t, IDX_HEADS, IDX_DIM)
    u = ga * jax.nn.sigmoid(gb)
    return q, k, v, qi, ki, wi, u


def indexer_scores(qi, wi, ki):
    dots = jnp.einsum('bthd,bsd->bths', qi, ki, preferred_element_type=jnp.float32) * (IDX_DIM ** -0.5)
    return jnp.einsum('bths,bth->bts', jax.nn.relu(dots), wi.astype(jnp.float32) * (IDX_HEADS ** -0.5))


def sparse_attention(q, k_all, v_all, scores, q_pos, topk, slopes):
    key_pos = jnp.arange(k_all.shape[1])
    admissible = (key_pos[None, :] // CHUNK) <= (q_pos[:, None] // CHUNK)
    _, idx = lax.top_k(jnp.where(admissible[None], scores, NEG_INF), topk)
    valid = (idx // CHUNK) <= (q_pos[None, :, None] // CHUNK)
    gather_rows = jax.vmap(lambda rows, ids: rows[ids])
    k_sel = gather_rows(k_all, idx)
    v_sel = gather_rows(v_all, idx)
    logits = jnp.einsum('bthd,btkhd->bhtk', q, k_sel, preferred_element_type=jnp.float32) * (HEAD_DIM ** -0.5)
    dist = jnp.abs(q_pos[None, :, None] - idx).astype(jnp.float32)
    logits = logits - slopes[None, :, None, None] * dist[:, None]
    logits = jnp.where(valid[:, None], logits, NEG_INF)
    p = jax.nn.softmax(logits, axis=-1)
    out = jnp.einsum('bhtk,btkhd->bthd', p.astype(v_sel.dtype), v_sel, preferred_element_type=jnp.float32)
    return out.astype(q.dtype)


def conv_module(u_ext, conv_w, conv_b, ln_g, ln_b):
    y = lax.conv_general_dilated(u_ext, conv_w[:, None, :].astype(u_ext.dtype), window_strides=(1,),
                                 padding='VALID', dimension_numbers=('NWC', 'WIO', 'NWC'),
                                 feature_group_count=CONV_DIM) + conv_b
    return jax.nn.silu(layer_norm(y, ln_g, ln_b))


def prompt_mixer(h, w_in, conv_w, conv_b, ln_g, ln_b, slopes):
    b, s, _ = h.shape
    q, k, v, qi, ki, wi, u = split_projection(h, w_in)
    topk = min(TOPK_MAX, s // 4)

    def query_block(j):
        start = j * Q_BLOCK
        sl = lambda a: lax.dynamic_slice_in_dim(a, start, Q_BLOCK, axis=1)
        q_pos = start + jnp.arange(Q_BLOCK)
        sc = indexer_scores(sl(qi), sl(wi), ki)
        return sparse_attention(sl(q), k, v, sc, q_pos, topk, slopes)

    attn = lax.map(query_block, jnp.arange(s // Q_BLOCK))
    attn = jnp.moveaxis(attn, 0, 1).reshape(b, s, ATTN_DIM)
    u_ext = jnp.pad(u, ((0, 0), (CONV_WIDTH - 1, 0), (0, 0)))
    conv = conv_module(u_ext, conv_w, conv_b, ln_g, ln_b)
    mix = jnp.concatenate([attn, conv], axis=-1)
    return mix, (k, v, ki, u_ext[:, -(CONV_WIDTH - 1):])


def sample_mixer(h, cache_k, cache_v, cache_kidx, state_conv, w_in, conv_w, conv_b, ln_g, ln_b, slopes):
    b, t, _ = h.shape
    past = cache_k.shape[1]
    q, k, v, qi, ki, wi, u = split_projection(h, w_in)
    k_all = jnp.concatenate([cache_k, k], axis=1)
    v_all = jnp.concatenate([cache_v, v], axis=1)
    ki_all = jnp.concatenate([cache_kidx, ki], axis=1)
    topk = min(TOPK_MAX, (past + t) // 4)
    q_pos = past + jnp.arange(t)
    sc = indexer_scores(qi, wi, ki_all)
    attn = sparse_attention(q, k_all, v_all, sc, q_pos, topk, slopes).reshape(b, t, ATTN_DIM)
    u_ext = jnp.concatenate([state_conv.astype(u.dtype), u], axis=1)
    conv = conv_module(u_ext, conv_w, conv_b, ln_g, ln_b)
    mix = jnp.concatenate([attn, conv], axis=-1)
    return mix, (k, v, ki, u_ext[:, -(CONV_WIDTH - 1):])


def residual_block(x, c, mixer, w_ada, b_ada, norm1_g, w_out, norm2_g, w_ff1, w_ff2):
    sh1, sc1, g1, sh2, sc2, g2 = ada_modulation(c, w_ada, b_ada)
    h = rms_norm(x, norm1_g) * (1 + sc1) + sh1
    mix, state = mixer(h)
    x = x + g1 * (mix @ w_out)
    h = rms_norm(x, norm2_g) * (1 + sc2) + sh2
    x = x + g2 * (jnp.square(jax.nn.relu(h @ w_ff1)) @ w_ff2)
    return x, state


def setup_inputs(seed: int = 0) -> dict:
    key = jax.random.key(seed)
    ks = jax.random.split(key, 24)
    nrm = lambda k, shape, scale: jax.random.normal(k, shape, jnp.float32) * scale
    return {
        "x_prompt": nrm(ks[0], (BATCH, SEQ, D_MODEL), 1.0),
        "x_sample": nrm(ks[1], (DEC_BATCH, DEC_SEQ, D_MODEL), 1.0),
        "c_prompt": nrm(ks[2], (BATCH, D_MODEL), 1.0),
        "c_sample": nrm(ks[3], (DEC_BATCH, D_MODEL), 1.0),
        "cache_k": nrm(ks[4], (DEPTH, DEC_BATCH, PAST_LEN, N_HEADS, HEAD_DIM), 1.0),
        "cache_v": nrm(ks[5], (DEPTH, DEC_BATCH, PAST_LEN, N_HEADS, HEAD_DIM), 1.0),
        "cache_kidx": nrm(ks[6], (DEPTH, DEC_BATCH, PAST_LEN, IDX_DIM), 1.0),
        "state_conv": nrm(ks[7], (DEPTH, DEC_BATCH, CONV_WIDTH - 1, CONV_DIM), 0.5),
        "w_ada": nrm(ks[8], (DEPTH, D_MODEL, 6 * D_MODEL), 0.5 * D_MODEL ** -0.5),
        "b_ada": nrm(ks[9], (DEPTH, 6 * D_MODEL), 0.02),
        "norm1_g": 1.0 + nrm(ks[10], (DEPTH, D_MODEL), 0.02),
        "w_in": nrm(ks[11], (DEPTH, D_MODEL, PROJ_DIM), D_MODEL ** -0.5),
        "conv_w": nrm(ks[12], (DEPTH, CONV_WIDTH, CONV_DIM), CONV_WIDTH ** -0.5),
        "conv_b": nrm(ks[13], (DEPTH, CONV_DIM), 0.02),
        "conv_ln_g": 1.0 + nrm(ks[14], (DEPTH, CONV_DIM), 0.02),
        "conv_ln_b": nrm(ks[15], (DEPTH, CONV_DIM), 0.02),
        "w_out": nrm(ks[16], (DEPTH, D_MIX, D_MODEL), D_MIX ** -0.5),
        "norm2_g": 1.0 + nrm(ks[17], (DEPTH, D_MODEL), 0.02),
        "w_ff1": nrm(ks[18], (DEPTH, D_MODEL, D_FF), D_MODEL ** -0.5),
        "w_ff2": nrm(ks[19], (DEPTH, D_FF, D_MODEL), D_FF ** -0.5),
        "final_g": 1.0 + nrm(ks[20], (D_MODEL,), 0.02),
    }


def reference(x_prompt, x_sample, c_prompt, c_sample, cache_k, cache_v, cache_kidx, state_conv,
              w_ada, b_ada, norm1_g, w_in, conv_w, conv_b, conv_ln_g, conv_ln_b, w_out, norm2_g,
              w_ff1, w_ff2, final_g):
    slopes = alibi_slopes()
    yp, ys = x_prompt, x_sample
    kp_l, vp_l, kip_l, cp_l, ks_l, vs_l, kis_l, cs_l = [], [], [], [], [], [], [], []
    for l in range(DEPTH):
        pmix = lambda h: prompt_mixer(h, w_in[l], conv_w[l], conv_b[l], conv_ln_g[l], conv_ln_b[l], slopes)
        yp, (kp, vp, kip, cp) = residual_block(yp, c_prompt, pmix, w_ada[l], b_ada[l], norm1_g[l],
                                               w_out[l], norm2_g[l], w_ff1[l], w_ff2[l])
        smix = lambda h: sample_mixer(h, cache_k[l], cache_v[l], cache_kidx[l], state_conv[l], w_in[l],
                                      conv_w[l], conv_b[l], conv_ln_g[l], conv_ln_b[l], slopes)
        ys, (kss, vss, kis, css) = residual_block(ys, c_sample, smix, w_ada[l], b_ada[l], norm1_g[l],
                                                  w_out[l], norm2_g[l], w_ff1[l], w_ff2[l])
        kp_l.append(kp); vp_l.append(vp); kip_l.append(kip); cp_l.append(cp)
        ks_l.append(kss); vs_l.append(vss); kis_l.append(kis); cs_l.append(css)
    y_prompt = rms_norm(yp, final_g)
    y_sample = rms_norm(ys, final_g)
    return (y_prompt, y_sample,
            jnp.stack(kp_l), jnp.stack(vp_l), jnp.stack(kip_l), jnp.stack(cp_l),
            jnp.stack(ks_l), jnp.stack(vs_l), jnp.stack(kis_l), jnp.stack(cs_l))
```

```python
import functools

import jax
import jax.numpy as jnp
from jax import lax
from jax.experimental import pallas as pl
from jax.experimental.pallas import tpu as pltpu

F32 = jnp.float32
BF16 = jnp.bfloat16
I32 = jnp.int32

CHUNK = 64
N_HEADS = 8
HEAD_DIM = 64
ATTN_DIM = N_HEADS * HEAD_DIM
IDX_HEADS = 8
IDX_DIM = 64
TOPK_MAX = 256
CONV_WIDTH = 31
EPS = 1e-6
NEG_INF = -1e30
MASKED_DIST = 1e34
F32_LOWEST = -3.0e38
INT_MAX = 2 ** 31 - 1
LANES = 128
SUBLANES = 8
CTX_ROWS = 32
VMEM_LIMIT = 56 * 1024 * 1024
BISECT_STEPS_PER_CHECK = 4

_NT = (((1,), (1,)), ((), ()))
_SLOPES = tuple(2.0 ** (-8.0 * (h + 1) / N_HEADS) for h in range(N_HEADS))
_WSCALE = (IDX_DIM ** -0.5) * (IDX_HEADS ** -0.5)


def _sigmoid(x):
    return 1.0 / (1.0 + jnp.exp(-x))


def _bdot(a, b):
    return jnp.dot(a, b, preferred_element_type=F32)


def _ntdot(a, b):
    return lax.dot_general(a, b, _NT, preferred_element_type=F32)


def _ada_kernel(c_ref, w_ref, b_ref, o_ref):
    c = c_ref[...]
    a = (c * _sigmoid(c)).astype(BF16)
    o_ref[...] = _bdot(a, w_ref[...].astype(BF16)) + b_ref[...]


def _ada_call(c_all, w_ada, b_ada):
    n, d = c_all.shape
    nout = w_ada.shape[1]
    tn = 1536
    assert nout % tn == 0
    return pl.pallas_call(
        _ada_kernel,
        grid=(nout // tn,),
        in_specs=[
            pl.BlockSpec((n, d), lambda i: (0, 0)),
            pl.BlockSpec((d, tn), lambda i: (0, i)),
            pl.BlockSpec((1, tn), lambda i: (0, i)),
        ],
        out_specs=pl.BlockSpec((n, tn), lambda i: (0, i)),
        out_shape=jax.ShapeDtypeStruct((n, nout), F32),
        compiler_params=pltpu.CompilerParams(vmem_limit_bytes=VMEM_LIMIT),
        name="ada",
    )(c_all, w_ada, b_ada.reshape(1, nout))


_T_Q, _T_QI, _T_K, _T_V, _T_KI = 0, ATTN_DIM, 2 * ATTN_DIM, 3 * ATTN_DIM, 4 * ATTN_DIM
_T_SMALL = 80
_T_ROWS = _T_KI + _T_SMALL


def _inproj_kernel(x_ref, mod_ref, g_ref, wrow_ref, wt_ref, cw_ref, cb_ref, lg_ref, lb_ref, cinit_ref,
                   qt_ref, qit_ref, kt_ref, vt_ref, vtb_ref, kit_ref, wit_ref, kb_ref, mb_ref, conv_ref, cst_ref,
                   uext_ref, *, tm, kt_w, d_model, conv_dim):
    j = pl.program_id(1)
    nt = pl.num_programs(1)
    x = x_ref[0]
    sh1 = mod_ref[0, :, 0:d_model]
    sc1 = mod_ref[0, :, d_model:2 * d_model]
    y = x * lax.rsqrt(jnp.mean(x * x, axis=-1, keepdims=True) + EPS)
    h = (y * g_ref[...]) * (1.0 + sc1) + sh1
    hb = h.astype(BF16)

    a = ATTN_DIM
    qt_ref[0] = (_ntdot(wt_ref[_T_Q:_T_Q + a, :], hb) * (HEAD_DIM ** -0.5)).astype(BF16)
    qit_ref[0] = _ntdot(wt_ref[_T_QI:_T_QI + a, :], hb).astype(BF16)
    kt_ref[0] = _ntdot(wt_ref[_T_K:_T_K + a, :], hb)
    vt = _ntdot(wt_ref[_T_V:_T_V + a, :], hb)
    vt_ref[0] = vt
    vtb = vt.astype(BF16)
    for t in range(tm // kt_w):
        vtb_ref[0, t] = vtb[:, t * kt_w:(t + 1) * kt_w]
    small = _ntdot(wt_ref[_T_KI:_T_KI + _T_SMALL, :], hb)
    kit_ref[0] = small[0:IDX_DIM, :]
    wit_ref[0] = small[IDX_DIM:IDX_DIM + IDX_HEADS, :] * _WSCALE

    kb_ref[0] = _bdot(hb, wrow_ref[:, 0:a]).astype(BF16)
    mb_ref[0] = _bdot(hb, wrow_ref[:, a:a + LANES]).astype(BF16)
    o = a + LANES
    ga = _bdot(hb, wrow_ref[:, o:o + conv_dim])
    gb = _bdot(hb, wrow_ref[:, o + conv_dim:o + 2 * conv_dim])
    u = ga * _sigmoid(gb)

    @pl.when(j == 0)
    def _():
        uext_ref[0:CTX_ROWS, :] = cinit_ref[0]

    uext_ref[CTX_ROWS:CTX_ROWS + tm, :] = u
    off = CTX_ROWS - (CONV_WIDTH - 1)
    rc = min(tm, 64)
    for r in range(tm // rc):
        acc = jnp.broadcast_to(cb_ref[...], (rc, conv_dim))
        for kk in range(CONV_WIDTH):
            acc = acc + uext_ref[pl.ds(off + r * rc + kk, rc), :] * cw_ref[kk:kk + 1, :]
        mu = jnp.mean(acc, axis=-1, keepdims=True)
        xc = acc - mu
        var = jnp.mean(xc * xc, axis=-1, keepdims=True)
        yn = xc * lax.rsqrt(var + EPS) * lg_ref[...] + lb_ref[...]
        conv_ref[0, r * rc:(r + 1) * rc, :] = (yn * _sigmoid(yn)).astype(BF16)

    @pl.when(j == nt - 1)
    def _():
        cst_ref[0] = uext_ref[pl.ds(tm + off, CONV_WIDTH - 1), :]

    uext_ref[0:CTX_ROWS, :] = uext_ref[tm:tm + CTX_ROWS, :]


def _inproj_call(x, mod, norm_g, w_row, w_t, conv_w, conv_b, ln_g, ln_b, conv_init, *, tm, kt_w):
    b, s, d = x.shape
    conv_dim = conv_w.shape[1]
    assert s % tm == 0 and tm % kt_w == 0 and tm >= CTX_ROWS
    grid = (b, s // tm)
    row = lambda bi, j: (bi, j, 0)
    col = lambda bi, j: (bi, 0, j)
    const2 = lambda bi, j: (0, 0)
    kernel = functools.partial(_inproj_kernel, tm=tm, kt_w=kt_w, d_model=d, conv_dim=conv_dim)
    out_shape = (
        jax.ShapeDtypeStruct((b, ATTN_DIM, s), BF16),
        jax.ShapeDtypeStruct((b, IDX_HEADS * IDX_DIM, s), BF16),
        jax.ShapeDtypeStruct((b, ATTN_DIM, s), F32),
        jax.ShapeDtypeStruct((b, ATTN_DIM, s), F32),
        jax.ShapeDtypeStruct((b, s // kt_w, ATTN_DIM, kt_w), BF16),
        jax.ShapeDtypeStruct((b, IDX_DIM, s), F32),
        jax.ShapeDtypeStruct((b, IDX_HEADS, s), F32),
        jax.ShapeDtypeStruct((b, s, ATTN_DIM), BF16),
        jax.ShapeDtypeStruct((b, s, LANES), BF16),
        jax.ShapeDtypeStruct((b, s, conv_dim), BF16),
        jax.ShapeDtypeStruct((b, CONV_WIDTH - 1, conv_dim), F32),
    )
    out_specs = (
        pl.BlockSpec((1, ATTN_DIM, tm), col),
        pl.BlockSpec((1, IDX_HEADS * IDX_DIM, tm), col),
        pl.BlockSpec((1, ATTN_DIM, tm), col),
        pl.BlockSpec((1, ATTN_DIM, tm), col),
        pl.BlockSpec((1, tm // kt_w, ATTN_DIM, kt_w), lambda bi, j: (bi, j, 0, 0)),
        pl.BlockSpec((1, IDX_DIM, tm), col),
        pl.BlockSpec((1, IDX_HEADS, tm), col),
        pl.BlockSpec((1, tm, ATTN_DIM), row),
        pl.BlockSpec((1, tm, LANES), row),
        pl.BlockSpec((1, tm, conv_dim), row),
        pl.BlockSpec((1, CONV_WIDTH - 1, conv_dim), lambda bi, j: (bi, 0, 0)),
    )
    in_specs = [
        pl.BlockSpec((1, tm, d), row),
        pl.BlockSpec((1, 1, mod.shape[2]), lambda bi, j: (bi, 0, 0)),
        pl.BlockSpec((1, d), const2),
        pl.BlockSpec(w_row.shape, const2),
        pl.BlockSpec(w_t.shape, const2),
        pl.BlockSpec((CONV_WIDTH, conv_dim), const2),
        pl.BlockSpec((1, conv_dim), const2),
        pl.BlockSpec((1, conv_dim), const2),
        pl.BlockSpec((1, conv_dim), const2),
        pl.BlockSpec((1, CTX_ROWS, conv_dim), lambda bi, j: (bi, 0, 0)),
    ]
    return pl.pallas_call(
        kernel,
        grid=grid,
        in_specs=in_specs,
        out_specs=out_specs,
        out_shape=out_shape,
        scratch_shapes=[pltpu.VMEM((CTX_ROWS + tm, conv_dim), F32)],
        compiler_params=pltpu.CompilerParams(
            dimension_semantics=("arbitrary", "arbitrary"), vmem_limit_bytes=VMEM_LIMIT),
        name="inproj",
    )(x, mod, norm_g.reshape(1, d), w_row, w_t, conv_w, conv_b.reshape(1, conv_dim),
      ln_g.reshape(1, conv_dim), ln_b.reshape(1, conv_dim), conv_init)


def _bisect_threshold(count_ge, bound, topk):
    kf = float(topk)

    def step(lo, hi):
        mid = 0.5 * (lo + hi)
        cnt = count_ge(mid)
        ge = cnt >= kf
        lo = jnp.where(ge, mid, lo)
        hi = jnp.where(cnt == kf, mid, jnp.where(ge, hi, mid))
        return lo, hi

    def body(c):
        lo, hi, _ = c
        for _ in range(BISECT_STEPS_PER_CHECK):
            lo, hi = step(lo, hi)
        mid = 0.5 * (lo + hi)
        active = jnp.where((mid != lo) & (mid != hi), 1.0, 0.0)
        return lo, hi, (jnp.max(active) > 0.0).astype(I32)

    lo, _, _ = lax.while_loop(lambda c: c[2] > 0, body, (-bound, bound, jnp.int32(1)))
    return lo


def _tie_index_bound(count, thr, topk, idx_bits, zeros_i32):
    kf = float(topk)
    need = kf - count(lambda s, idx: s > thr)

    def jbody(i, jb):
        cand = jb | lax.shift_left(jnp.int32(1), jnp.int32(idx_bits - 1) - i)
        cnt = count(lambda s, idx: (s == thr) & (idx < cand))
        return jnp.where(cnt < need, cand, jb)

    return lax.fori_loop(0, idx_bits, jbody, zeros_i32)


def _selected(s, idx, thr, jb):
    return (s > thr) | ((s == thr) & (idx <= jb))


def _col_reduce(x, op):
    rows, lanes = x.shape
    part = op(x.reshape(rows // SUBLANES, SUBLANES, lanes), axis=0)
    return op(part, axis=0, keepdims=True)


def _attn_kernel(qt_ref, qit_ref, wit_ref, kb_ref, mb_ref, vtb_ref, o_ref,
                 s_ref, qpad_ref, qipad_ref, amax_ref, thr_ref, jb_ref, pen_ref, lg_ref, m_ref, l_ref, acc_ref,
                 *, tq, kt_w, nblk, topk):
    j = pl.program_id(1)

    zeros_h = jnp.zeros((HEAD_DIM, tq), BF16)
    for h in range(N_HEADS):
        qh = qt_ref[0, h * HEAD_DIM:(h + 1) * HEAD_DIM, :]
        qpad_ref[h] = jnp.concatenate([qh, zeros_h] if h % 2 == 0 else [zeros_h, qh], axis=0)
        qipad_ref[h] = jnp.concatenate([qit_ref[0, h * IDX_DIM:(h + 1) * IDX_DIM, :], zeros_h], axis=0)

    row_i = lax.broadcasted_iota(I32, (kt_w, tq), 0)
    col_i = lax.broadcasted_iota(I32, (kt_w, tq), 1)
    amax_ref[...] = jnp.zeros(amax_ref.shape, F32)

    def score_tile(kt, diagonal):
        mk = mb_ref[0, pl.ds(pl.multiple_of(kt * kt_w, kt_w), kt_w), :]
        s = jnp.zeros((kt_w, tq), F32)
        for h in range(IDX_HEADS):
            s = s + jnp.maximum(_bdot(mk, qipad_ref[h]), 0.0) * wit_ref[0, h:h + 1, :]
        mag = jnp.abs(s)
        if diagonal:
            adm = (row_i // CHUNK) <= (col_i // CHUNK)
            s = jnp.where(adm, s, -jnp.inf)
            mag = jnp.where(adm, mag, 0.0)
        amax_ref[...] = jnp.maximum(amax_ref[...], jnp.max(mag.reshape(kt_w // SUBLANES, SUBLANES, tq), axis=0))
        s_ref[kt] = s

    def score_body(kt, carry):
        score_tile(kt, False)
        return carry

    lax.fori_loop(0, j, score_body, 0)
    score_tile(j, True)

    thr_ref[...] = jnp.full(thr_ref.shape, F32_LOWEST, F32)
    jb_ref[...] = jnp.full(jb_ref.shape, INT_MAX, I32)
    idx_bits = (nblk * kt_w - 1).bit_length()
    for c in range(nblk):
        if (c + 1) * tq <= topk:
            continue

        @pl.when(j == c)
        def _(c=c):
            def count(pred):
                tot = jnp.zeros((SUBLANES, tq), F32)
                for t in range(c + 1):
                    hit = jnp.where(pred(s_ref[t], t * kt_w + row_i), 1.0, 0.0)
                    tot = tot + jnp.sum(hit.reshape(kt_w // SUBLANES, SUBLANES, tq), axis=0)
                return jnp.sum(tot, axis=0, keepdims=True)

            bound = jnp.max(amax_ref[...], axis=0, keepdims=True) + 1.0
            thr = _bisect_threshold(lambda t: count(lambda s, idx: s >= t), bound, topk)
            thr_ref[...] = thr

            @pl.when(jnp.max(count(lambda s, idx: s >= thr)) > float(topk))
            def _():
                jb_ref[...] = _tie_index_bound(count, thr, topk, idx_bits, jnp.zeros((1, tq), I32))

    m_ref[...] = jnp.full(m_ref.shape, NEG_INF, F32)
    l_ref[...] = jnp.zeros(l_ref.shape, F32)
    acc_ref[...] = jnp.zeros(acc_ref.shape, F32)
    qpos = j * tq + col_i

    def attn_body(kt, carry):
        kpos = kt * kt_w + row_i
        sel = _selected(s_ref[kt], kpos, thr_ref[...], jb_ref[...])
        pen_ref[...] = jnp.where(sel, jnp.abs(qpos - kpos).astype(F32), MASKED_DIST)
        rows = pl.ds(pl.multiple_of(kt * kt_w, kt_w), kt_w)
        for h in range(N_HEADS):
            pr = h // 2
            lg_ref[h] = _bdot(kb_ref[0, rows, pr * LANES:(pr + 1) * LANES], qpad_ref[h])
        for h in range(N_HEADS):
            lg = lg_ref[h] - _SLOPES[h] * pen_ref[...]
            m_prev = m_ref[h:h + 1, :]
            m_new = jnp.maximum(m_prev, _col_reduce(lg, jnp.max))
            alpha = jnp.exp(m_prev - m_new)
            p = jnp.exp(lg - m_new)
            l_ref[h:h + 1, :] = alpha * l_ref[h:h + 1, :] + _col_reduce(p, jnp.sum)
            m_ref[h:h + 1, :] = m_new
            hs = slice(h * HEAD_DIM, (h + 1) * HEAD_DIM)
            acc_ref[hs, :] = acc_ref[hs, :] * alpha + _bdot(vtb_ref[0, kt, hs, :], p.astype(BF16))
        return carry

    lax.fori_loop(0, j + 1, attn_body, 0)
    out_t = jnp.concatenate(
        [acc_ref[h * HEAD_DIM:(h + 1) * HEAD_DIM, :] / l_ref[h:h + 1, :] for h in range(N_HEADS)], axis=0)
    o_ref[0] = jnp.transpose(out_t).astype(BF16)


def _attn_call(qt, qit, wit, kb, mb, vtb, *, tq, topk):
    b, _, s = qt.shape
    nkt, _, kt_w = vtb.shape[1:]
    assert kt_w == tq and s % tq == 0 and tq % CHUNK == 0 and tq % LANES == 0
    nblk = s // tq
    kernel = functools.partial(_attn_kernel, tq=tq, kt_w=kt_w, nblk=nblk, topk=topk)
    col = lambda bi, j: (bi, 0, j)
    whole = lambda bi, j: (bi, 0, 0)
    return pl.pallas_call(
        kernel,
        grid=(b, nblk),
        in_specs=[
            pl.BlockSpec((1, ATTN_DIM, tq), col),
            pl.BlockSpec((1, IDX_HEADS * IDX_DIM, tq), col),
            pl.BlockSpec((1, IDX_HEADS, tq), col),
            pl.BlockSpec((1, s, ATTN_DIM), whole),
            pl.BlockSpec((1, s, LANES), whole),
            pl.BlockSpec((1, nkt, ATTN_DIM, kt_w), lambda bi, j: (bi, 0, 0, 0)),
        ],
        out_specs=pl.BlockSpec((1, tq, ATTN_DIM), lambda bi, j: (bi, j, 0)),
        out_shape=jax.ShapeDtypeStruct((b, s, ATTN_DIM), BF16),
        scratch_shapes=[
            pltpu.VMEM((nkt, kt_w, tq), F32),
            pltpu.VMEM((N_HEADS, 2 * HEAD_DIM, tq), BF16),
            pltpu.VMEM((IDX_HEADS, 2 * IDX_DIM, tq), BF16),
            pltpu.VMEM((SUBLANES, tq), F32),
            pltpu.VMEM((1, tq), F32),
            pltpu.VMEM((1, tq), I32),
            pltpu.VMEM((kt_w, tq), F32),
            pltpu.VMEM((N_HEADS, kt_w, tq), F32),
            pltpu.VMEM((N_HEADS, tq), F32),
            pltpu.VMEM((N_HEADS, tq), F32),
            pltpu.VMEM((ATTN_DIM, tq), F32),
        ],
        compiler_params=pltpu.CompilerParams(
            dimension_semantics=("arbitrary", "arbitrary"), vmem_limit_bytes=VMEM_LIMIT),
        name="attn",
    )(qt, qit, wit, kb, mb, vtb)


def _sattn_kernel(q_ref, qi_ref, wi_ref, ktn_ref, vtn_ref, kitn_ref, ckt_ref, cvt_ref, ckit_ref, o_ref,
                  s_ref, sn_ref, wb_ref, amax_ref, thr_ref, jb_ref, m_ref, l_ref, acc_ref,
                  *, t_new, lt, ntile, past, topk):
    ph = pl.program_id(1)
    t = pl.program_id(2)
    nch = lt // LANES

    def lanes(a, width):
        return jnp.concatenate([a] * (width // LANES), axis=1) if width >= LANES else a[:, 0:width]

    @pl.when((ph == 0) & (t == 0))
    def _():
        wi = wi_ref[0]
        for h in range(IDX_HEADS):
            wb_ref[h] = jnp.broadcast_to(wi[:, h:h + 1], (t_new, LANES))
        amax_ref[...] = jnp.zeros(amax_ref.shape, F32)

    def scores(kit, width):
        s = jnp.zeros((t_new, width), F32)
        for h in range(IDX_HEADS):
            d = _bdot(qi_ref[0, :, h * IDX_DIM:(h + 1) * IDX_DIM], kit)
            s = s + jnp.maximum(d, 0.0) * lanes(wb_ref[h], width)
        return s

    @pl.when(ph == 0)
    def _():
        s = scores(ckit_ref[0].astype(BF16), lt)
        mag = jnp.abs(s)
        am = amax_ref[...]
        for c in range(nch):
            am = jnp.maximum(am, mag[:, c * LANES:(c + 1) * LANES])
        amax_ref[...] = am
        s_ref[t] = s

    @pl.when((ph == 0) & (t == ntile - 1))
    def _():
        sn = scores(kitn_ref[0].astype(BF16), t_new)
        sn_ref[...] = jnp.full((t_new, LANES), -jnp.inf, F32)
        sn_ref[:, 0:t_new] = sn
        pad0 = jnp.zeros((t_new, LANES - t_new), F32)
        amax = jnp.maximum(amax_ref[...], jnp.concatenate([jnp.abs(sn), pad0], axis=1))
        lane = lax.broadcasted_iota(I32, (t_new, LANES), 1)

        def count(pred):
            cnt = jnp.zeros((t_new, LANES), F32)
            for tt in range(ntile):
                for c in range(nch):
                    base = tt * lt + c * LANES
                    cnt = cnt + jnp.where(pred(s_ref[tt, :, c * LANES:(c + 1) * LANES], base + lane), 1.0, 0.0)
            cnt = cnt + jnp.where(pred(sn_ref[...], past + lane), 1.0, 0.0)
            return jnp.sum(cnt, axis=-1, keepdims=True)

        bound = jnp.broadcast_to(jnp.max(amax, axis=-1, keepdims=True) + 1.0, (t_new, LANES))
        thr = _bisect_threshold(lambda v: count(lambda s, idx: s >= v), bound, topk)
        thr_ref[...] = thr
        jb_ref[...] = jnp.full((t_new, LANES), INT_MAX, I32)

        @pl.when(jnp.max(count(lambda s, idx: s >= thr)) > float(topk))
        def _():
            jb_ref[...] = _tie_index_bound(count, thr, topk, (past + LANES - 1).bit_length(),
                                           jnp.zeros((t_new, LANES), I32))

        m_ref[...] = jnp.full(m_ref.shape, NEG_INF, F32)
        l_ref[...] = jnp.zeros(l_ref.shape, F32)
        acc_ref[...] = jnp.zeros(acc_ref.shape, F32)

    lane_lo = lax.broadcasted_iota(I32, (t_new, LANES), 1) < HEAD_DIM

    def attend(kt_b, vt_b, s, kpos0, width):
        kpos = kpos0 + lax.broadcasted_iota(I32, (t_new, width), 1)
        qpos = past + lax.broadcasted_iota(I32, (t_new, width), 0)
        sel = _selected(s, kpos, lanes(thr_ref[...], width), lanes(jb_ref[...], width))
        pen = jnp.where(sel, jnp.abs(qpos - kpos).astype(F32), MASKED_DIST)
        for hp in range(N_HEADS // 2):
            vp = vt_b[hp * LANES:(hp + 1) * LANES, :]
            pv = []
            alpha = []
            for h in (2 * hp, 2 * hp + 1):
                hs = slice(h * HEAD_DIM, (h + 1) * HEAD_DIM)
                lg = _bdot(q_ref[0, :, hs], kt_b[hs, :]) - _SLOPES[h] * pen
                m_prev = m_ref[h]
                m_new = jnp.maximum(m_prev, jnp.max(lg, axis=-1, keepdims=True))
                a = jnp.exp(m_prev - m_new)
                p = jnp.exp(lg - lanes(m_new, width))
                l_ref[h] = a * l_ref[h] + jnp.sum(p, axis=-1, keepdims=True)
                m_ref[h] = m_new
                pv.append(_ntdot(p.astype(BF16), vp))
                alpha.append(a)
            cols = slice(hp * LANES, (hp + 1) * LANES)
            acc_ref[:, cols] = (acc_ref[:, cols] * jnp.where(lane_lo, alpha[0], alpha[1])
                                + jnp.where(lane_lo, pv[0], pv[1]))

    @pl.when(ph == 1)
    def _():
        attend(ckt_ref[0].astype(BF16), cvt_ref[0].astype(BF16), s_ref[t], t * lt, lt)

    @pl.when((ph == 1) & (t == ntile - 1))
    def _():
        attend(ktn_ref[0].astype(BF16), vtn_ref[0].astype(BF16), sn_ref[:, 0:t_new], past, t_new)
        for hp in range(N_HEADS // 2):
            cols = slice(hp * LANES, (hp + 1) * LANES)
            lsum = jnp.where(lane_lo, l_ref[2 * hp], l_ref[2 * hp + 1])
            o_ref[0, :, cols] = (acc_ref[:, cols] / lsum).astype(BF16)


def _sattn_call(q, qi, wi, ktn, vtn, kitn, cache_kt, cache_vt, cache_kit, *, lt, topk):
    b, t_new, _ = q.shape
    past = cache_kt.shape[2]
    assert past % lt == 0 and lt % LANES == 0 and t_new <= LANES and t_new % SUBLANES == 0
    assert past % CHUNK == 0 and t_new <= CHUNK
    ntile = past // lt
    kernel = functools.partial(_sattn_kernel, t_new=t_new, lt=lt, ntile=ntile, past=past, topk=topk)
    fixed = lambda bi, ph, t: (bi, 0, 0)
    return pl.pallas_call(
        kernel,
        grid=(b, 2, ntile),
        in_specs=[
            pl.BlockSpec((1, t_new, ATTN_DIM), fixed),
            pl.BlockSpec((1, t_new, IDX_HEADS * IDX_DIM), fixed),
            pl.BlockSpec((1, t_new, IDX_HEADS), fixed),
            pl.BlockSpec((1, ATTN_DIM, t_new), fixed),
            pl.BlockSpec((1, ATTN_DIM, t_new), fixed),
            pl.BlockSpec((1, IDX_DIM, t_new), fixed),
            pl.BlockSpec((1, ATTN_DIM, lt), lambda bi, ph, t: (bi, 0, ph * t)),
            pl.BlockSpec((1, ATTN_DIM, lt), lambda bi, ph, t: (bi, 0, ph * t)),
            pl.BlockSpec((1, IDX_DIM, lt), lambda bi, ph, t: (bi, 0, jnp.where(ph == 0, t, ntile - 1))),
        ],
        out_specs=pl.BlockSpec((1, t_new, ATTN_DIM), fixed),
        out_shape=jax.ShapeDtypeStruct((b, t_new, ATTN_DIM), BF16),
        scratch_shapes=[
            pltpu.VMEM((ntile, t_new, lt), F32),
            pltpu.VMEM((t_new, LANES), F32),
            pltpu.VMEM((IDX_HEADS, t_new, LANES), F32),
            pltpu.VMEM((t_new, LANES), F32),
            pltpu.VMEM((t_new, LANES), F32),
            pltpu.VMEM((t_new, LANES), I32),
            pltpu.VMEM((N_HEADS, t_new, LANES), F32),
            pltpu.VMEM((N_HEADS, t_new, LANES), F32),
            pltpu.VMEM((t_new, ATTN_DIM), F32),
        ],
        compiler_params=pltpu.CompilerParams(
            dimension_semantics=("arbitrary", "arbitrary", "arbitrary"), vmem_limit_bytes=VMEM_LIMIT),
        name="sattn",
    )(q, qi, wi, ktn, vtn, kitn, cache_kt, cache_vt, cache_kit)


def _outmlp_kernel(x_ref, attn_ref, conv_ref, mod_ref, wo_ref, g2_ref, w1_ref, w2_ref, gf_ref, o_ref,
                   *, d_model, fchunk):
    bb, sb, _ = x_ref.shape
    rows = bb * sb
    md = lambda i: mod_ref[:, :, i * d_model:(i + 1) * d_model]
    gate1, sh2, sc2, gate2 = md(2), md(3), md(4), md(5)
    a = attn_ref[...].reshape(rows, ATTN_DIM)
    c = conv_ref[...].reshape(rows, conv_ref.shape[2])
    mix = _bdot(a, wo_ref[0:ATTN_DIM, :]) + _bdot(c, wo_ref[ATTN_DIM:, :])
    x1 = x_ref[...] + gate1 * mix.reshape(bb, sb, d_model)
    y = x1 * lax.rsqrt(jnp.mean(x1 * x1, axis=-1, keepdims=True) + EPS)
    h2 = ((y * g2_ref[...]) * (1.0 + sc2) + sh2).reshape(rows, d_model).astype(BF16)
    ff = jnp.zeros((rows, d_model), F32)
    for f in range(w1_ref.shape[1] // fchunk):
        hid = jnp.maximum(_bdot(h2, w1_ref[:, f * fchunk:(f + 1) * fchunk]), 0.0)
        ff = ff + _bdot((hid * hid).astype(BF16), w2_ref[f * fchunk:(f + 1) * fchunk, :])
    x2 = x1 + gate2 * ff.reshape(bb, sb, d_model)
    o_ref[...] = x2 * lax.rsqrt(jnp.mean(x2 * x2, axis=-1, keepdims=True) + EPS) * gf_ref[...]


def _outmlp_call(x, attn, conv, mod, w_out, norm2_g, w_ff1, w_ff2, final_g, *, bb, sb):
    b, s, d = x.shape
    dff = w_ff1.shape[1]
    assert b % bb == 0 and s % sb == 0
    kernel = functools.partial(_outmlp_kernel, d_model=d, fchunk=1024)
    blk = lambda bi, j: (bi, j, 0)
    const2 = lambda bi, j: (0, 0)
    resident = dict(pipeline_mode=pl.Buffered(1))
    return pl.pallas_call(
        kernel,
        grid=(b // bb, s // sb),
        in_specs=[
            pl.BlockSpec((bb, sb, d), blk),
            pl.BlockSpec((bb, sb, attn.shape[2]), blk),
            pl.BlockSpec((bb, sb, conv.shape[2]), blk),
            pl.BlockSpec((bb, 1, mod.shape[2]), lambda bi, j: (bi, 0, 0)),
            pl.BlockSpec(w_out.shape, const2, **resident),
            pl.BlockSpec((1, d), const2),
            pl.BlockSpec((d, dff), const2, **resident),
            pl.BlockSpec((dff, d), const2, **resident),
            pl.BlockSpec((1, d), const2),
        ],
        out_specs=pl.BlockSpec((bb, sb, d), blk),
        out_shape=jax.ShapeDtypeStruct((b, s, d), F32),
        compiler_params=pltpu.CompilerParams(
            dimension_semantics=("arbitrary", "arbitrary"), vmem_limit_bytes=VMEM_LIMIT),
        name="outmlp",
    )(x, attn, conv, mod, w_out, norm2_g.reshape(1, d), w_ff1, w_ff2, final_g.reshape(1, d))


def _pack_in_proj(w_in, conv_dim):
    a = ATTN_DIM
    o = 4 * a
    small = IDX_DIM + IDX_HEADS
    d = w_in.shape[0]
    w_row = jnp.concatenate(
        [w_in[:, a:2 * a], w_in[:, o:o + small], jnp.zeros((d, LANES - small), w_in.dtype), w_in[:, o + small:]],
        axis=1).astype(BF16)
    w_t = jnp.concatenate(
        [w_in[:, 0:a], w_in[:, 3 * a:4 * a], w_in[:, a:3 * a], w_in[:, o:o + small],
         jnp.zeros((d, _T_SMALL - small), w_in.dtype)], axis=1).T.astype(BF16)
    assert w_row.shape[1] == a + LANES + 2 * conv_dim and w_t.shape[0] == _T_ROWS
    return w_row, w_t


def _positions_last(a):
    b, _, s = a.shape
    return jnp.transpose(a.reshape(1, b, N_HEADS, HEAD_DIM, s), (0, 1, 4, 2, 3))


def _channels_first(a):
    b, p = a.shape[:2]
    return jnp.transpose(a, (0, 2, 3, 1)).reshape(b, -1, p)


def kernel(x_prompt, x_sample, c_prompt, c_sample, cache_k, cache_v, cache_kidx, state_conv, w_ada, b_ada,
           norm1_g, w_in, conv_w, conv_b, conv_ln_g, conv_ln_b, w_out, norm2_g, w_ff1, w_ff2, final_g):
    depth = w_in.shape[0]
    assert depth == 1
    b, s, d = x_prompt.shape
    db, ds, _ = x_sample.shape
    past = cache_k.shape[2]
    conv_dim = conv_w.shape[2]
    l = 0

    mod = _ada_call(jnp.concatenate([c_prompt, c_sample], axis=0), w_ada[l], b_ada[l])
    mod_p = mod[:b].reshape(b, 1, -1)
    mod_s = mod[b:].reshape(db, 1, -1)

    w_row, w_t = _pack_in_proj(w_in[l], conv_dim)
    wo = w_out[l].astype(BF16)
    w1 = w_ff1[l].astype(BF16)
    w2 = w_ff2[l].astype(BF16)
    conv_args = (conv_w[l], conv_b[l], conv_ln_g[l], conv_ln_b[l])

    tq = 256
    zero_ctx = jnp.zeros((b, CTX_ROWS, conv_dim), F32)
    qt, qit, ktp, vtp, vtb, kitp, wit, kb, mb, convp, cstp = _inproj_call(
        x_prompt, mod_p, norm1_g[l], w_row, w_t, *conv_args, zero_ctx, tm=512, kt_w=tq)
    attn_p = _attn_call(qt, qit, wit, kb, mb, vtb, tq=tq, topk=min(TOPK_MAX, s // 4))
    y_prompt = _outmlp_call(x_prompt, attn_p, convp, mod_p, wo, norm2_g[l], w1, w2, final_g, bb=1, sb=512)

    ctx_s = jnp.pad(state_conv[l].astype(F32), ((0, 0), (CTX_ROWS - (CONV_WIDTH - 1), 0), (0, 0)))
    qts, qits, kts, vts, _, kits, wits, _, _, convs, csts = _inproj_call(
        x_sample, mod_s, norm1_g[l], w_row, w_t, *conv_args, ctx_s, tm=ds, kt_w=ds)
    rows = lambda a: jnp.transpose(a, (0, 2, 1))
    attn_s = _sattn_call(rows(qts), rows(qits), rows(wits), kts, vts, kits,
                         _channels_first(cache_k[l]), _channels_first(cache_v[l]), rows(cache_kidx[l]),
                         lt=1024, topk=min(TOPK_MAX, (past + ds) // 4))
    y_sample = _outmlp_call(x_sample, attn_s, convs, mod_s, wo, norm2_g[l], w1, w2, final_g, bb=4, sb=ds)

    return (y_prompt, y_sample, _positions_last(ktp), _positions_last(vtp), rows(kitp)[None], cstp[None],
            _positions_last(kts), _positions_last(vts), rows(kits)[None], csts[None])
```

```python
import functools

import jax
import jax.numpy as jnp
from jax import lax
from jax.experimental import pallas as pl
from jax.experimental.pallas import tpu as pltpu

F32 = jnp.float32
BF16 = jnp.bfloat16
I32 = jnp.int32

CHUNK = 64
N_HEADS = 8
HEAD_DIM = 64
ATTN_DIM = N_HEADS * HEAD_DIM
IDX_HEADS = 8
IDX_DIM = 64
TOPK_MAX = 256
CONV_WIDTH = 31
EPS = 1e-6
NEG_INF = -1e30
MASKED_DIST = 1e34
F32_LOWEST = -3.0e38
INT_MAX = 2 ** 31 - 1
LANES = 128
SUBLANES = 8
CTX_ROWS = 32
VMEM_LIMIT = 56 * 1024 * 1024
BISECT_STEPS_PER_CHECK = 4

_NT = (((1,), (1,)), ((), ()))
_SLOPES = tuple(2.0 ** (-8.0 * (h + 1) / N_HEADS) for h in range(N_HEADS))
_WSCALE = (IDX_DIM ** -0.5) * (IDX_HEADS ** -0.5)


def _sigmoid(x):
    return 1.0 / (1.0 + jnp.exp(-x))


def _bdot(a, b):
    return jnp.dot(a, b, preferred_element_type=F32)


def _ntdot(a, b):
    return lax.dot_general(a, b, _NT, preferred_element_type=F32)


def _ada_kernel(c_ref, w_ref, b_ref, o_ref):
    c = c_ref[...]
    a = (c * _sigmoid(c)).astype(BF16)
    o_ref[...] = _bdot(a, w_ref[...].astype(BF16)) + b_ref[...]


def _ada_call(c_all, w_ada, b_ada):
    n, d = c_all.shape
    nout = w_ada.shape[1]
    tn = 1536
    assert nout % tn == 0
    return pl.pallas_call(
        _ada_kernel,
        grid=(nout // tn,),
        in_specs=[
            pl.BlockSpec((n, d), lambda i: (0, 0)),
            pl.BlockSpec((d, tn), lambda i: (0, i)),
            pl.BlockSpec((1, tn), lambda i: (0, i)),
        ],
        out_specs=pl.BlockSpec((n, tn), lambda i: (0, i)),
        out_shape=jax.ShapeDtypeStruct((n, nout), F32),
        compiler_params=pltpu.CompilerParams(vmem_limit_bytes=VMEM_LIMIT),
        name="ada",
    )(c_all, w_ada, b_ada.reshape(1, nout))


_T_Q, _T_QI, _T_K, _T_V, _T_KI = 0, ATTN_DIM, 2 * ATTN_DIM, 3 * ATTN_DIM, 4 * ATTN_DIM
_T_SMALL = 80
_T_ROWS = _T_KI + _T_SMALL


def _inproj_kernel(x_ref, mod_ref, g_ref, wrow_ref, wt_ref, cw_ref, cb_ref, lg_ref, lb_ref, cinit_ref,
                   qt_ref, qit_ref, kt_ref, vt_ref, vtb_ref, kit_ref, wit_ref, kb_ref, mb_ref, conv_ref, cst_ref,
                   uext_ref, shift_ref, *, tm, kt_w, d_model, conv_dim):
    j = pl.program_id(1)
    nt = pl.num_programs(1)
    x = x_ref[0]
    sh1 = mod_ref[0, :, 0:d_model]
    sc1 = mod_ref[0, :, d_model:2 * d_model]
    y = x * lax.rsqrt(jnp.mean(x * x, axis=-1, keepdims=True) + EPS)
    h = (y * g_ref[...]) * (1.0 + sc1) + sh1
    hb = h.astype(BF16)

    a = ATTN_DIM
    qt_ref[0] = (_ntdot(wt_ref[_T_Q:_T_Q + a, :], hb) * (HEAD_DIM ** -0.5)).astype(BF16)
    qit_ref[0] = _ntdot(wt_ref[_T_QI:_T_QI + a, :], hb).astype(BF16)
    kt_ref[0] = _ntdot(wt_ref[_T_K:_T_K + a, :], hb)
    vt = _ntdot(wt_ref[_T_V:_T_V + a, :], hb)
    vt_ref[0] = vt
    vtb = vt.astype(BF16)
    for t in range(tm // kt_w):
        vtb_ref[0, t] = vtb[:, t * kt_w:(t + 1) * kt_w]
    small = _ntdot(wt_ref[_T_KI:_T_KI + _T_SMALL, :], hb)
    kit_ref[0] = small[0:IDX_DIM, :]
    wit_ref[0] = small[IDX_DIM:IDX_DIM + IDX_HEADS, :] * _WSCALE

    kb_ref[0] = _bdot(hb, wrow_ref[:, 0:a]).astype(BF16)
    mb_ref[0] = _bdot(hb, wrow_ref[:, a:a + LANES]).astype(BF16)
    o = a + LANES
    ga = _bdot(hb, wrow_ref[:, o:o + conv_dim])
    gb = _bdot(hb, wrow_ref[:, o + conv_dim:o + 2 * conv_dim])
    u = ga * _sigmoid(gb)

    @pl.when(j == 0)
    def _():
        uext_ref[0:CTX_ROWS, :] = cinit_ref[0]

    uext_ref[CTX_ROWS:CTX_ROWS + tm, :] = u
    off = CTX_ROWS - (CONV_WIDTH - 1)
    rc = min(tm, 64)
    for r in range(tm // rc):
        sh = shift_ref.at[r % 2]
        for res in range(SUBLANES):
            span = rc + SUBLANES * ((CONV_WIDTH - 1 - res) // SUBLANES)
            sh[res, 0:span, :] = uext_ref[pl.ds(off + r * rc + res, span), :]
        acc = jnp.broadcast_to(cb_ref[...], (rc, conv_dim))
        for kk in range(CONV_WIDTH):
            a0 = SUBLANES * (kk // SUBLANES)
            acc = acc + sh[kk % SUBLANES, a0:a0 + rc, :] * cw_ref[kk:kk + 1, :]
        mu = jnp.mean(acc, axis=-1, keepdims=True)
        xc = acc - mu
        var = jnp.mean(xc * xc, axis=-1, keepdims=True)
        yn = xc * lax.rsqrt(var + EPS) * lg_ref[...] + lb_ref[...]
        conv_ref[0, r * rc:(r + 1) * rc, :] = (yn * _sigmoid(yn)).astype(BF16)

    @pl.when(j == nt - 1)
    def _():
        cst_ref[0] = uext_ref[pl.ds(tm + off, CONV_WIDTH - 1), :]

    uext_ref[0:CTX_ROWS, :] = uext_ref[tm:tm + CTX_ROWS, :]


def _inproj_call(x, mod, norm_g, w_row, w_t, conv_w, conv_b, ln_g, ln_b, conv_init, *, tm, kt_w):
    b, s, d = x.shape
    conv_dim = conv_w.shape[1]
    assert s % tm == 0 and tm % kt_w == 0 and tm >= CTX_ROWS
    grid = (b, s // tm)
    row = lambda bi, j: (bi, j, 0)
    col = lambda bi, j: (bi, 0, j)
    const2 = lambda bi, j: (0, 0)
    kernel = functools.partial(_inproj_kernel, tm=tm, kt_w=kt_w, d_model=d, conv_dim=conv_dim)
    out_shape = (
        jax.ShapeDtypeStruct((b, ATTN_DIM, s), BF16),
        jax.ShapeDtypeStruct((b, IDX_HEADS * IDX_DIM, s), BF16),
        jax.ShapeDtypeStruct((b, ATTN_DIM, s), F32),
        jax.ShapeDtypeStruct((b, ATTN_DIM, s), F32),
        jax.ShapeDtypeStruct((b, s // kt_w, ATTN_DIM, kt_w), BF16),
        jax.ShapeDtypeStruct((b, IDX_DIM, s), F32),
        jax.ShapeDtypeStruct((b, IDX_HEADS, s), F32),
        jax.ShapeDtypeStruct((b, s, ATTN_DIM), BF16),
        jax.ShapeDtypeStruct((b, s, LANES), BF16),
        jax.ShapeDtypeStruct((b, s, conv_dim), BF16),
        jax.ShapeDtypeStruct((b, CONV_WIDTH - 1, conv_dim), F32),
    )
    out_specs = (
        pl.BlockSpec((1, ATTN_DIM, tm), col),
        pl.BlockSpec((1, IDX_HEADS * IDX_DIM, tm), col),
        pl.BlockSpec((1, ATTN_DIM, tm), col),
        pl.BlockSpec((1, ATTN_DIM, tm), col),
        pl.BlockSpec((1, tm // kt_w, ATTN_DIM, kt_w), lambda bi, j: (bi, j, 0, 0)),
        pl.BlockSpec((1, IDX_DIM, tm), col),
        pl.BlockSpec((1, IDX_HEADS, tm), col),
        pl.BlockSpec((1, tm, ATTN_DIM), row),
        pl.BlockSpec((1, tm, LANES), row),
        pl.BlockSpec((1, tm, conv_dim), row),
        pl.BlockSpec((1, CONV_WIDTH - 1, conv_dim), lambda bi, j: (bi, 0, 0)),
    )
    in_specs = [
        pl.BlockSpec((1, tm, d), row),
        pl.BlockSpec((1, 1, mod.shape[2]), lambda bi, j: (bi, 0, 0)),
        pl.BlockSpec((1, d), const2),
        pl.BlockSpec(w_row.shape, const2),
        pl.BlockSpec(w_t.shape, const2),
        pl.BlockSpec((CONV_WIDTH, conv_dim), const2),
        pl.BlockSpec((1, conv_dim), const2),
        pl.BlockSpec((1, conv_dim), const2),
        pl.BlockSpec((1, conv_dim), const2),
        pl.BlockSpec((1, CTX_ROWS, conv_dim), lambda bi, j: (bi, 0, 0)),
    ]
    return pl.pallas_call(
        kernel,
        grid=grid,
        in_specs=in_specs,
        out_specs=out_specs,
        out_shape=out_shape,
        scratch_shapes=[
            pltpu.VMEM((CTX_ROWS + tm, conv_dim), F32),
            pltpu.VMEM((2, SUBLANES, min(tm, 64) + SUBLANES * ((CONV_WIDTH - 1) // SUBLANES), conv_dim), F32),
        ],
        compiler_params=pltpu.CompilerParams(
            dimension_semantics=("arbitrary", "arbitrary"), vmem_limit_bytes=VMEM_LIMIT),
        name="inproj",
    )(x, mod, norm_g.reshape(1, d), w_row, w_t, conv_w, conv_b.reshape(1, conv_dim),
      ln_g.reshape(1, conv_dim), ln_b.reshape(1, conv_dim), conv_init)


def _bisect_threshold(count, bound, topk):
    kf = float(topk)
    n_pos = count(lambda s, idx: s > 0.0)
    n_nonneg = count(lambda s, idx: s >= 0.0)
    positive = n_pos >= kf
    zero = (n_nonneg >= kf) & jnp.logical_not(positive)
    lo0 = jnp.where(positive | zero, 0.0, -bound)
    hi0 = jnp.where(zero, 0.0, jnp.where(positive, bound, 0.0))

    def step(lo, hi):
        mid = 0.5 * (lo + hi)
        cnt = count(lambda s, idx: s >= mid)
        ge = cnt >= kf
        lo = jnp.where(ge, mid, lo)
        hi = jnp.where(cnt == kf, mid, jnp.where(ge, hi, mid))
        return lo, hi

    def body(c):
        lo, hi, _ = c
        for _ in range(BISECT_STEPS_PER_CHECK):
            lo, hi = step(lo, hi)
        mid = 0.5 * (lo + hi)
        active = jnp.where((mid != lo) & (mid != hi), 1.0, 0.0)
        return lo, hi, (jnp.max(active) > 0.0).astype(I32)

    lo, _, _ = lax.while_loop(lambda c: c[2] > 0, body, (lo0, hi0, jnp.int32(1)))
    return lo


def _tie_index_bound(count, thr, topk, idx_bits, zeros_i32):
    kf = float(topk)
    need = kf - count(lambda s, idx: s > thr)

    def jbody(i, jb):
        cand = jb | lax.shift_left(jnp.int32(1), jnp.int32(idx_bits - 1) - i)
        cnt = count(lambda s, idx: (s == thr) & (idx < cand))
        return jnp.where(cnt < need, cand, jb)

    return lax.fori_loop(0, idx_bits, jbody, zeros_i32)


def _selected(s, idx, thr, jb):
    return (s > thr) | ((s == thr) & (idx <= jb))


def _col_reduce(x, op):
    rows, lanes = x.shape
    part = op(x.reshape(rows // SUBLANES, SUBLANES, lanes), axis=0)
    return op(part, axis=0, keepdims=True)


def _attn_kernel(qt_ref, qit_ref, wit_ref, kb_ref, mb_ref, vtb_ref, o_ref,
                 s_ref, qpad_ref, qipad_ref, amax_ref, thr_ref, jb_ref, pen_ref, lg_ref, m_ref, l_ref, acc_ref,
                 *, tq, kt_w, nblk, topk):
    j = pl.program_id(1)

    zeros_h = jnp.zeros((HEAD_DIM, tq), BF16)
    for h in range(N_HEADS):
        qh = qt_ref[0, h * HEAD_DIM:(h + 1) * HEAD_DIM, :]
        qpad_ref[h] = jnp.concatenate([qh, zeros_h] if h % 2 == 0 else [zeros_h, qh], axis=0)
        qipad_ref[h] = jnp.concatenate([qit_ref[0, h * IDX_DIM:(h + 1) * IDX_DIM, :], zeros_h], axis=0)

    row_i = lax.broadcasted_iota(I32, (kt_w, tq), 0)
    col_i = lax.broadcasted_iota(I32, (kt_w, tq), 1)
    amax_ref[...] = jnp.zeros(amax_ref.shape, F32)

    def score_tile(kt, diagonal):
        mk = mb_ref[0, pl.ds(pl.multiple_of(kt * kt_w, kt_w), kt_w), :]
        s = jnp.zeros((kt_w, tq), F32)
        for h in range(IDX_HEADS):
            s = s + jnp.maximum(_bdot(mk, qipad_ref[h]), 0.0) * wit_ref[0, h:h + 1, :]
        mag = jnp.abs(s)
        if diagonal:
            adm = (row_i // CHUNK) <= (col_i // CHUNK)
            s = jnp.where(adm, s, -jnp.inf)
            mag = jnp.where(adm, mag, 0.0)
        amax_ref[...] = jnp.maximum(amax_ref[...], jnp.max(mag.reshape(kt_w // SUBLANES, SUBLANES, tq), axis=0))
        s_ref[kt] = s

    def score_body(kt, carry):
        score_tile(kt, False)
        return carry

    lax.fori_loop(0, j, score_body, 0)
    score_tile(j, True)

    thr_ref[...] = jnp.full(thr_ref.shape, F32_LOWEST, F32)
    jb_ref[...] = jnp.full(jb_ref.shape, INT_MAX, I32)
    idx_bits = (nblk * kt_w - 1).bit_length()
    for c in range(nblk):
        if (c + 1) * tq <= topk:
            continue

        @pl.when(j == c)
        def _(c=c):
            def count(pred):
                tot = jnp.zeros((SUBLANES, tq), F32)
                for t in range(c + 1):
                    hit = jnp.where(pred(s_ref[t], t * kt_w + row_i), 1.0, 0.0)
                    tot = tot + jnp.sum(hit.reshape(kt_w // SUBLANES, SUBLANES, tq), axis=0)
                return jnp.sum(tot, axis=0, keepdims=True)

            bound = jnp.max(amax_ref[...], axis=0, keepdims=True) + 1.0
            thr = _bisect_threshold(count, bound, topk)
            thr_ref[...] = thr

            @pl.when(jnp.max(count(lambda s, idx: s >= thr)) > float(topk))
            def _():
                jb_ref[...] = _tie_index_bound(count, thr, topk, idx_bits, jnp.zeros((1, tq), I32))

    m_ref[...] = jnp.full(m_ref.shape, NEG_INF, F32)
    l_ref[...] = jnp.zeros(l_ref.shape, F32)
    acc_ref[...] = jnp.zeros(acc_ref.shape, F32)
    qpos = j * tq + col_i

    def attn_body(kt, carry):
        kpos = kt * kt_w + row_i
        sel = _selected(s_ref[kt], kpos, thr_ref[...], jb_ref[...])
        pen_ref[...] = jnp.where(sel, jnp.abs(qpos - kpos).astype(F32), MASKED_DIST)
        rows = pl.ds(pl.multiple_of(kt * kt_w, kt_w), kt_w)
        for h in range(N_HEADS):
            pr = h // 2
            lg_ref[h] = _bdot(kb_ref[0, rows, pr * LANES:(pr + 1) * LANES], qpad_ref[h])
        for h in range(N_HEADS):
            lg = lg_ref[h] - _SLOPES[h] * pen_ref[...]
            m_prev = m_ref[h:h + 1, :]
            m_new = jnp.maximum(m_prev, _col_reduce(lg, jnp.max))
            alpha = jnp.exp(m_prev - m_new)
            p = jnp.exp(lg - m_new)
            l_ref[h:h + 1, :] = alpha * l_ref[h:h + 1, :] + _col_reduce(p, jnp.sum)
            m_ref[h:h + 1, :] = m_new
            hs = slice(h * HEAD_DIM, (h + 1) * HEAD_DIM)
            acc_ref[hs, :] = acc_ref[hs, :] * alpha + _bdot(vtb_ref[0, kt, hs, :], p.astype(BF16))
        return carry

    lax.fori_loop(0, j + 1, attn_body, 0)
    out_t = jnp.concatenate(
        [acc_ref[h * HEAD_DIM:(h + 1) * HEAD_DIM, :] / l_ref[h:h + 1, :] for h in range(N_HEADS)], axis=0)
    o_ref[0] = jnp.transpose(out_t).astype(BF16)


def _attn_call(qt, qit, wit, kb, mb, vtb, *, tq, topk):
    b, _, s = qt.shape
    nkt, _, kt_w = vtb.shape[1:]
    assert kt_w == tq and s % tq == 0 and tq % CHUNK == 0 and tq % LANES == 0
    nblk = s // tq
    kernel = functools.partial(_attn_kernel, tq=tq, kt_w=kt_w, nblk=nblk, topk=topk)
    col = lambda bi, j: (bi, 0, j)
    whole = lambda bi, j: (bi, 0, 0)
    return pl.pallas_call(
        kernel,
        grid=(b, nblk),
        in_specs=[
            pl.BlockSpec((1, ATTN_DIM, tq), col),
            pl.BlockSpec((1, IDX_HEADS * IDX_DIM, tq), col),
            pl.BlockSpec((1, IDX_HEADS, tq), col),
            pl.BlockSpec((1, s, ATTN_DIM), whole),
            pl.BlockSpec((1, s, LANES), whole),
            pl.BlockSpec((1, nkt, ATTN_DIM, kt_w), lambda bi, j: (bi, 0, 0, 0)),
        ],
        out_specs=pl.BlockSpec((1, tq, ATTN_DIM), lambda bi, j: (bi, j, 0)),
        out_shape=jax.ShapeDtypeStruct((b, s, ATTN_DIM), BF16),
        scratch_shapes=[
            pltpu.VMEM((nkt, kt_w, tq), F32),
            pltpu.VMEM((N_HEADS, 2 * HEAD_DIM, tq), BF16),
            pltpu.VMEM((IDX_HEADS, 2 * IDX_DIM, tq), BF16),
            pltpu.VMEM((SUBLANES, tq), F32),
            pltpu.VMEM((1, tq), F32),
            pltpu.VMEM((1, tq), I32),
            pltpu.VMEM((kt_w, tq), F32),
            pltpu.VMEM((N_HEADS, kt_w, tq), F32),
            pltpu.VMEM((N_HEADS, tq), F32),
            pltpu.VMEM((N_HEADS, tq), F32),
            pltpu.VMEM((ATTN_DIM, tq), F32),
        ],
        compiler_params=pltpu.CompilerParams(
            dimension_semantics=("arbitrary", "arbitrary"), vmem_limit_bytes=VMEM_LIMIT),
        name="attn",
    )(qt, qit, wit, kb, mb, vtb)


def _sattn_kernel(q_ref, qi_ref, wi_ref, ktn_ref, vtn_ref, kitn_ref, ckt_ref, cvt_ref, ckit_ref, o_ref,
                  s_ref, sn_ref, wb_ref, amax_ref, thr_ref, jb_ref, m_ref, l_ref, acc_ref,
                  *, t_new, lt, ntile, past, topk):
    ph = pl.program_id(1)
    t = pl.program_id(2)
    nch = lt // LANES

    def lanes(a, width):
        return jnp.concatenate([a] * (width // LANES), axis=1) if width >= LANES else a[:, 0:width]

    @pl.when((ph == 0) & (t == 0))
    def _():
        wi = wi_ref[0]
        for h in range(IDX_HEADS):
            wb_ref[h] = jnp.broadcast_to(wi[:, h:h + 1], (t_new, LANES))
        amax_ref[...] = jnp.zeros(amax_ref.shape, F32)

    def scores(kit, width):
        s = jnp.zeros((t_new, width), F32)
        for h in range(IDX_HEADS):
            d = _bdot(qi_ref[0, :, h * IDX_DIM:(h + 1) * IDX_DIM], kit)
            s = s + jnp.maximum(d, 0.0) * lanes(wb_ref[h], width)
        return s

    @pl.when(ph == 0)
    def _():
        s = scores(ckit_ref[0].astype(BF16), lt)
        mag = jnp.abs(s)
        am = amax_ref[...]
        for c in range(nch):
            am = jnp.maximum(am, mag[:, c * LANES:(c + 1) * LANES])
        amax_ref[...] = am
        s_ref[t] = s

    @pl.when((ph == 0) & (t == ntile - 1))
    def _():
        sn = scores(kitn_ref[0].astype(BF16), t_new)
        sn_ref[...] = jnp.full((t_new, LANES), -jnp.inf, F32)
        sn_ref[:, 0:t_new] = sn
        pad0 = jnp.zeros((t_new, LANES - t_new), F32)
        amax = jnp.maximum(amax_ref[...], jnp.concatenate([jnp.abs(sn), pad0], axis=1))
        lane = lax.broadcasted_iota(I32, (t_new, LANES), 1)

        def count(pred):
            cnt = jnp.zeros((t_new, LANES), F32)
            for tt in range(ntile):
                for c in range(nch):
                    base = tt * lt + c * LANES
                    cnt = cnt + jnp.where(pred(s_ref[tt, :, c * LANES:(c + 1) * LANES], base + lane), 1.0, 0.0)
            cnt = cnt + jnp.where(pred(sn_ref[...], past + lane), 1.0, 0.0)
            return jnp.sum(cnt, axis=-1, keepdims=True)

        bound = jnp.broadcast_to(jnp.max(amax, axis=-1, keepdims=True) + 1.0, (t_new, LANES))
        thr = _bisect_threshold(count, bound, topk)
        thr_ref[...] = thr
        jb_ref[...] = jnp.full((t_new, LANES), INT_MAX, I32)

        @pl.when(jnp.max(count(lambda s, idx: s >= thr)) > float(topk))
        def _():
            jb_ref[...] = _tie_index_bound(count, thr, topk, (past + LANES - 1).bit_length(),
                                           jnp.zeros((t_new, LANES), I32))

        m_ref[...] = jnp.full(m_ref.shape, NEG_INF, F32)
        l_ref[...] = jnp.zeros(l_ref.shape, F32)
        acc_ref[...] = jnp.zeros(acc_ref.shape, F32)

    lane_lo = lax.broadcasted_iota(I32, (t_new, LANES), 1) < HEAD_DIM

    def attend(kt_b, vt_b, s, kpos0, width):
        kpos = kpos0 + lax.broadcasted_iota(I32, (t_new, width), 1)
        qpos = past + lax.broadcasted_iota(I32, (t_new, width), 0)
        sel = _selected(s, kpos, lanes(thr_ref[...], width), lanes(jb_ref[...], width))
        pen = jnp.where(sel, jnp.abs(qpos - kpos).astype(F32), MASKED_DIST)
        for hp in range(N_HEADS // 2):
            vp = vt_b[hp * LANES:(hp + 1) * LANES, :]
            pv = []
            alpha = []
            for h in (2 * hp, 2 * hp + 1):
                hs = slice(h * HEAD_DIM, (h + 1) * HEAD_DIM)
                lg = _bdot(q_ref[0, :, hs], kt_b[hs, :]) - _SLOPES[h] * pen
                m_prev = m_ref[h]
                m_new = jnp.maximum(m_prev, jnp.max(lg, axis=-1, keepdims=True))
                a = jnp.exp(m_prev - m_new)
                p = jnp.exp(lg - lanes(m_new, width))
                l_ref[h] = a * l_ref[h] + jnp.sum(p, axis=-1, keepdims=True)
                m_ref[h] = m_new
                pv.append(_ntdot(p.astype(BF16), vp))
                alpha.append(a)
            cols = slice(hp * LANES, (hp + 1) * LANES)
            acc_ref[:, cols] = (acc_ref[:, cols] * jnp.where(lane_lo, alpha[0], alpha[1])
                                + jnp.where(lane_lo, pv[0], pv[1]))

    @pl.when(ph == 1)
    def _():
        attend(ckt_ref[0].astype(BF16), cvt_ref[0].astype(BF16), s_ref[t], t * lt, lt)

    @pl.when((ph == 1) & (t == ntile - 1))
    def _():
        attend(ktn_ref[0].astype(BF16), vtn_ref[0].astype(BF16), sn_ref[:, 0:t_new], past, t_new)
        for hp in range(N_HEADS // 2):
            cols = slice(hp * LANES, (hp + 1) * LANES)
            lsum = jnp.where(lane_lo, l_ref[2 * hp], l_ref[2 * hp + 1])
            o_ref[0, :, cols] = (acc_ref[:, cols] / lsum).astype(BF16)


def _sattn_call(q, qi, wi, ktn, vtn, kitn, cache_kt, cache_vt, cache_kit, *, lt, topk):
    b, t_new, _ = q.shape
    past = cache_kt.shape[2]
    assert past % lt == 0 and lt % LANES == 0 and t_new <= LANES and t_new % SUBLANES == 0
    assert past % CHUNK == 0 and t_new <= CHUNK
    ntile = past // lt
    kernel = functools.partial(_sattn_kernel, t_new=t_new, lt=lt, ntile=ntile, past=past, topk=topk)
    fixed = lambda bi, ph, t: (bi, 0, 0)
    return pl.pallas_call(
        kernel,
        grid=(b, 2, ntile),
        in_specs=[
            pl.BlockSpec((1, t_new, ATTN_DIM), fixed),
            pl.BlockSpec((1, t_new, IDX_HEADS * IDX_DIM), fixed),
            pl.BlockSpec((1, t_new, IDX_HEADS), fixed),
            pl.BlockSpec((1, ATTN_DIM, t_new), fixed),
            pl.BlockSpec((1, ATTN_DIM, t_new), fixed),
            pl.BlockSpec((1, IDX_DIM, t_new), fixed),
            pl.BlockSpec((1, ATTN_DIM, lt), lambda bi, ph, t: (bi, 0, ph * t)),
            pl.BlockSpec((1, ATTN_DIM, lt), lambda bi, ph, t: (bi, 0, ph * t)),
            pl.BlockSpec((1, IDX_DIM, lt), lambda bi, ph, t: (bi, 0, jnp.where(ph == 0, t, ntile - 1))),
        ],
        out_specs=pl.BlockSpec((1, t_new, ATTN_DIM), fixed),
        out_shape=jax.ShapeDtypeStruct((b, t_new, ATTN_DIM), BF16),
        scratch_shapes=[
            pltpu.VMEM((ntile, t_new, lt), F32),
            pltpu.VMEM((t_new, LANES), F32),
            pltpu.VMEM((IDX_HEADS, t_new, LANES), F32),
            pltpu.VMEM((t_new, LANES), F32),
            pltpu.VMEM((t_new, LANES), F32),
            pltpu.VMEM((t_new, LANES), I32),
            pltpu.VMEM((N_HEADS, t_new, LANES), F32),
            pltpu.VMEM((N_HEADS, t_new, LANES), F32),
            pltpu.VMEM((t_new, ATTN_DIM), F32),
        ],
        compiler_params=pltpu.CompilerParams(
            dimension_semantics=("arbitrary", "arbitrary", "arbitrary"), vmem_limit_bytes=VMEM_LIMIT),
        name="sattn",
    )(q, qi, wi, ktn, vtn, kitn, cache_kt, cache_vt, cache_kit)


def _outmlp_kernel(x_ref, attn_ref, conv_ref, mod_ref, wo_ref, g2_ref, w1_ref, w2_ref, gf_ref, o_ref,
                   *, d_model, fchunk):
    bb, sb, _ = x_ref.shape
    rows = bb * sb
    md = lambda i: mod_ref[:, :, i * d_model:(i + 1) * d_model]
    gate1, sh2, sc2, gate2 = md(2), md(3), md(4), md(5)
    a = attn_ref[...].reshape(rows, ATTN_DIM)
    c = conv_ref[...].reshape(rows, conv_ref.shape[2])
    mix = _bdot(a, wo_ref[0:ATTN_DIM, :]) + _bdot(c, wo_ref[ATTN_DIM:, :])
    x1 = x_ref[...] + gate1 * mix.reshape(bb, sb, d_model)
    y = x1 * lax.rsqrt(jnp.mean(x1 * x1, axis=-1, keepdims=True) + EPS)
    h2 = ((y * g2_ref[...]) * (1.0 + sc2) + sh2).reshape(rows, d_model).astype(BF16)
    ff = jnp.zeros((rows, d_model), F32)
    for f in range(w1_ref.shape[1] // fchunk):
        hid = jnp.maximum(_bdot(h2, w1_ref[:, f * fchunk:(f + 1) * fchunk]), 0.0)
        ff = ff + _bdot((hid * hid).astype(BF16), w2_ref[f * fchunk:(f + 1) * fchunk, :])
    x2 = x1 + gate2 * ff.reshape(bb, sb, d_model)
    o_ref[...] = x2 * lax.rsqrt(jnp.mean(x2 * x2, axis=-1, keepdims=True) + EPS) * gf_ref[...]


def _outmlp_call(x, attn, conv, mod, w_out, norm2_g, w_ff1, w_ff2, final_g, *, bb, sb):
    b, s, d = x.shape
    dff = w_ff1.shape[1]
    assert b % bb == 0 and s % sb == 0
    kernel = functools.partial(_outmlp_kernel, d_model=d, fchunk=1024)
    blk = lambda bi, j: (bi, j, 0)
    const2 = lambda bi, j: (0, 0)
    resident = dict(pipeline_mode=pl.Buffered(1))
    return pl.pallas_call(
        kernel,
        grid=(b // bb, s // sb),
        in_specs=[
            pl.BlockSpec((bb, sb, d), blk),
            pl.BlockSpec((bb, sb, attn.shape[2]), blk),
            pl.BlockSpec((bb, sb, conv.shape[2]), blk),
            pl.BlockSpec((bb, 1, mod.shape[2]), lambda bi, j: (bi, 0, 0)),
            pl.BlockSpec(w_out.shape, const2, **resident),
            pl.BlockSpec((1, d), const2),
            pl.BlockSpec((d, dff), const2, **resident),
            pl.BlockSpec((dff, d), const2, **resident),
            pl.BlockSpec((1, d), const2),
        ],
        out_specs=pl.BlockSpec((bb, sb, d), blk),
        out_shape=jax.ShapeDtypeStruct((b, s, d), F32),
        compiler_params=pltpu.CompilerParams(
            dimension_semantics=("arbitrary", "arbitrary"), vmem_limit_bytes=VMEM_LIMIT),
        name="outmlp",
    )(x, attn, conv, mod, w_out, norm2_g.reshape(1, d), w_ff1, w_ff2, final_g.reshape(1, d))


def _pack_in_proj(w_in, conv_dim):
    a = ATTN_DIM
    o = 4 * a
    small = IDX_DIM + IDX_HEADS
    d = w_in.shape[0]
    w_row = jnp.concatenate(
        [w_in[:, a:2 * a], w_in[:, o:o + small], jnp.zeros((d, LANES - small), w_in.dtype), w_in[:, o + small:]],
        axis=1).astype(BF16)
    w_t = jnp.concatenate(
        [w_in[:, 0:a], w_in[:, 3 * a:4 * a], w_in[:, a:3 * a], w_in[:, o:o + small],
         jnp.zeros((d, _T_SMALL - small), w_in.dtype)], axis=1).T.astype(BF16)
    assert w_row.shape[1] == a + LANES + 2 * conv_dim and w_t.shape[0] == _T_ROWS
    return w_row, w_t


def _positions_last(a):
    b, _, s = a.shape
    return jnp.transpose(a.reshape(1, b, N_HEADS, HEAD_DIM, s), (0, 1, 4, 2, 3))


def _channels_first(a):
    b, p = a.shape[:2]
    return jnp.transpose(a, (0, 2, 3, 1)).reshape(b, -1, p)


def kernel(x_prompt, x_sample, c_prompt, c_sample, cache_k, cache_v, cache_kidx, state_conv, w_ada, b_ada,
           norm1_g, w_in, conv_w, conv_b, conv_ln_g, conv_ln_b, w_out, norm2_g, w_ff1, w_ff2, final_g):
    depth = w_in.shape[0]
    assert depth == 1
    b, s, d = x_prompt.shape
    db, ds, _ = x_sample.shape
    past = cache_k.shape[2]
    conv_dim = conv_w.shape[2]
    l = 0

    mod = _ada_call(jnp.concatenate([c_prompt, c_sample], axis=0), w_ada[l], b_ada[l])
    mod_p = mod[:b].reshape(b, 1, -1)
    mod_s = mod[b:].reshape(db, 1, -1)

    w_row, w_t = _pack_in_proj(w_in[l], conv_dim)
    wo = w_out[l].astype(BF16)
    w1 = w_ff1[l].astype(BF16)
    w2 = w_ff2[l].astype(BF16)
    conv_args = (conv_w[l], conv_b[l], conv_ln_g[l], conv_ln_b[l])

    tq = 256
    zero_ctx = jnp.zeros((b, CTX_ROWS, conv_dim), F32)
    qt, qit, ktp, vtp, vtb, kitp, wit, kb, mb, convp, cstp = _inproj_call(
        x_prompt, mod_p, norm1_g[l], w_row, w_t, *conv_args, zero_ctx, tm=512, kt_w=tq)
    attn_p = _attn_call(qt, qit, wit, kb, mb, vtb, tq=tq, topk=min(TOPK_MAX, s // 4))
    y_prompt = _outmlp_call(x_prompt, attn_p, convp, mod_p, wo, norm2_g[l], w1, w2, final_g, bb=1, sb=512)

    ctx_s = jnp.pad(state_conv[l].astype(F32), ((0, 0), (CTX_ROWS - (CONV_WIDTH - 1), 0), (0, 0)))
    qts, qits, kts, vts, _, kits, wits, _, _, convs, csts = _inproj_call(
        x_sample, mod_s, norm1_g[l], w_row, w_t, *conv_args, ctx_s, tm=ds, kt_w=ds)
    rows = lambda a: jnp.transpose(a, (0, 2, 1))
    attn_s = _sattn_call(rows(qts), rows(qits), rows(wits), kts, vts, kits,
                         _channels_first(cache_k[l]), _channels_first(cache_v[l]), rows(cache_kidx[l]),
                         lt=1024, topk=min(TOPK_MAX, (past + ds) // 4))
    y_sample = _outmlp_call(x_sample, attn_s, convs, mod_s, wo, norm2_g[l], w1, w2, final_g, bb=4, sb=ds)

    return (y_prompt, y_sample, _positions_last(ktp), _positions_last(vtp), rows(kitp)[None], cstp[None],
            _positions_last(kts), _positions_last(vts), rows(kits)[None], csts[None])
```
